```python
import jax, jax.numpy as jnp
from jax import lax
import numpy as np

D_MODEL = 1024
BATCH = 4
SEQ = 8192
DEPTH = 2

N_MIXERS = 2
EXPAND = 2
D_INNER = EXPAND * D_MODEL
CONV_WIDTH = 3
CHUNK = 128
GMLP_GROUPS = 8
GROUP_WIDTH = D_INNER // GMLP_GROUPS
N_CONV_LAYERS = (DEPTH + 1) // 2
N_GMLP_LAYERS = DEPTH // 2
RMS_EPS = 1e-6
LN_EPS = 1e-5

kernel_name = "hybrid_shortconv_chunked_gmlp_adaln"


def rms_norm(x, g):
    xf = x.astype(jnp.float32)
    y = xf * lax.rsqrt(jnp.mean(xf * xf, axis=-1, keepdims=True) + RMS_EPS)
    return (y * g.astype(jnp.float32)).astype(x.dtype)


def layer_norm(x, g, b):
    xf = x.astype(jnp.float32)
    mu = jnp.mean(xf, axis=-1, keepdims=True)
    var = jnp.mean(jnp.square(xf - mu), axis=-1, keepdims=True)
    y = (xf - mu) * lax.rsqrt(var + LN_EPS)
    return (y * g.astype(jnp.float32) + b.astype(jnp.float32)).astype(x.dtype)


def short_conv_mixer(h, w_in, conv_w, conv_b, w_out):
    seq = h.shape[1]
    proj = h @ w_in
    b_gate, c_gate, xin, z = jnp.split(proj, 4, axis=-1)
    cx = c_gate * xin
    padded = jnp.pad(cx, ((0, 0), (CONV_WIDTH - 1, 0), (0, 0)))
    conv = conv_b + conv_w[CONV_WIDTH - 1] * cx
    for k in range(CONV_WIDTH - 1):
        conv = conv + conv_w[k] * padded[:, k:k + seq]
    y = jax.nn.silu(z) * b_gate * conv
    return y @ w_out


def chunked_gmlp_mixer(h, w_in, ln_g, ln_b, w_s, b_s, w_out):
    bsz, seq, _ = h.shape
    proj = h @ w_in
    uv, z = proj[..., :2 * D_INNER], proj[..., 2 * D_INNER:]
    u, v = jnp.split(jax.nn.gelu(uv, approximate=False), 2, axis=-1)
    v = layer_norm(v, ln_g, ln_b)
    n_chunks = seq // CHUNK
    v = v.reshape(bsz, n_chunks, CHUNK, GMLP_GROUPS, GROUP_WIDTH)
    causal = jnp.tril(jnp.ones((CHUNK, CHUNK), dtype=bool))
    w = jnp.where(causal[None], w_s, jnp.zeros_like(w_s)).astype(v.dtype)
    mixed = jnp.einsum('gts,bnsgc->bntgc', w, v)
    mixed = mixed + jnp.transpose(b_s)[None, None, :, :, None].astype(v.dtype)
    s = u * mixed.reshape(bsz, seq, D_INNER)
    y = jax.nn.silu(z) * s
    return y @ w_out


def setup_inputs(seed: int = 0) -> dict:
    key = jax.random.key(seed)
    ks = jax.random.split(key, 20)
    nrm = jax.random.normal
    d, e = D_MODEL, D_INNER
    return {
        "x": nrm(ks[0], (BATCH, SEQ, d), jnp.float32),
        "c": nrm(ks[1], (BATCH, d), jnp.float32),
        "mod_w": nrm(ks[2], (DEPTH, d, 3 * d), jnp.float32) * (0.5 * d ** -0.5),
        "mod_b": nrm(ks[3], (DEPTH, 3 * d), jnp.float32) * 0.02,
        "norm_g": 1.0 + 0.05 * nrm(ks[4], (DEPTH, d), jnp.float32),
        "a_w_in": nrm(ks[5], (N_CONV_LAYERS, d, 4 * e), jnp.float32) * d ** -0.5,
        "a_conv_w": nrm(ks[6], (N_CONV_LAYERS, CONV_WIDTH, e), jnp.float32) * CONV_WIDTH ** -0.5,
        "a_conv_b": nrm(ks[7], (N_CONV_LAYERS, e), jnp.float32) * 0.02,
        "a_w_out": nrm(ks[8], (N_CONV_LAYERS, e, d), jnp.float32) * e ** -0.5,
        "b_w_in": nrm(ks[9], (N_GMLP_LAYERS, d, 3 * e), jnp.float32) * d ** -0.5,
        "b_ln_g": 1.0 + 0.05 * nrm(ks[10], (N_GMLP_LAYERS, e), jnp.float32),
        "b_ln_b": 0.02 * nrm(ks[11], (N_GMLP_LAYERS, e), jnp.float32),
        "b_w_s": nrm(ks[12], (N_GMLP_LAYERS, GMLP_GROUPS, CHUNK, CHUNK), jnp.float32) * CHUNK ** -0.5,
        "b_b_s": 1.0 + 0.1 * nrm(ks[13], (N_GMLP_LAYERS, GMLP_GROUPS, CHUNK), jnp.float32),
        "b_w_out": nrm(ks[14], (N_GMLP_LAYERS, e, d), jnp.float32) * e ** -0.5,
        "final_g": 1.0 + 0.05 * nrm(ks[15], (d,), jnp.float32),
    }


def reference(x, c, mod_w, mod_b, norm_g, a_w_in, a_conv_w, a_conv_b, a_w_out,
              b_w_in, b_ln_g, b_ln_b, b_w_s, b_b_s, b_w_out, final_g):
    c_act = jax.nn.silu(c)
    for i in range(DEPTH):
        mod = c_act @ mod_w[i] + mod_b[i]
        shift, scale, gate = jnp.split(mod[:, None, :], 3, axis=-1)
        h = rms_norm(x, norm_g[i]) * (1.0 + scale) + shift
        j = i // N_MIXERS
        if i % N_MIXERS == 0:
            branch = short_conv_mixer(h, a_w_in[j], a_conv_w[j], a_conv_b[j], a_w_out[j])
        else:
            branch = chunked_gmlp_mixer(h, b_w_in[j], b_ln_g[j], b_ln_b[j],
                                        b_w_s[j], b_b_s[j], b_w_out[j])
        x = x + gate * branch
    return rms_norm(x, final_g)
```

```python
import functools
import math

import jax
import jax.numpy as jnp
from jax import lax
from jax.experimental import pallas as pl
from jax.experimental.pallas import tpu as pltpu

D_MODEL = 1024
D_INNER = 2048
CONV_WIDTH = 3
CHUNK = 128
GMLP_GROUPS = 8
GROUP_WIDTH = D_INNER // GMLP_GROUPS
RMS_EPS = 1e-6
LN_EPS = 1e-5

SUBLANES = 8
SEQ_TILE = 512
CONV_COLS = 512
MOD_COLS = 1536
VMEM_LIMIT_BYTES = 56 * 1024 * 1024

_SQRT_HALF = math.sqrt(0.5)


def _silu(v):
    return v * jax.nn.sigmoid(v)


def _gelu_exact(v):
    return 0.5 * v * (1.0 + lax.erf(v * _SQRT_HALF))


def _rms_norm(x, g):
    ms = jnp.mean(x * x, axis=-1, keepdims=True)
    return (x * lax.rsqrt(ms + RMS_EPS)) * g


def _bdot(a, b):
    return jnp.dot(a, b, preferred_element_type=jnp.float32)


def _mod_kernel(c_ref, w_ref, b_ref, o_ref):
    c_act = _silu(c_ref[...])
    o_ref[0] = _bdot(c_act, w_ref[0]) + b_ref[0]


def _adaln_mod(c_pad, mod_w, mod_b):
    depth, d, n = mod_w.shape
    rows = c_pad.shape[0]
    return pl.pallas_call(
        _mod_kernel,
        grid=(depth, n // MOD_COLS),
        in_specs=[
            pl.BlockSpec((rows, d), lambda i, j: (0, 0)),
            pl.BlockSpec((1, d, MOD_COLS), lambda i, j: (i, 0, j)),
            pl.BlockSpec((1, 1, MOD_COLS), lambda i, j: (i, 0, j)),
        ],
        out_specs=pl.BlockSpec((1, rows, MOD_COLS), lambda i, j: (i, 0, j)),
        out_shape=jax.ShapeDtypeStruct((depth, rows, n), jnp.float32),
        compiler_params=pltpu.CompilerParams(
            dimension_semantics=("arbitrary", "arbitrary"),
            vmem_limit_bytes=VMEM_LIMIT_BYTES),
        name="adaln_mod",
    )(c_pad, mod_w, mod_b.reshape(depth, 1, n))


def _modulated_input(x, mod_ref, g_ref):
    shift = mod_ref[0, 0:1, :]
    scale = mod_ref[0, 1:2, :]
    return _rms_norm(x, g_ref[...]) * (1.0 + scale) + shift


def _conv_layer_kernel(x_ref, mod_ref, g_ref, win_ref, cw_ref, cb_ref, wout_ref,
                       o_ref, cx_ref, y_ref):
    ts = x_ref.shape[1]
    e = D_INNER

    @pl.when(pl.program_id(1) == 0)
    def _():
        cx_ref[0:SUBLANES, :] = jnp.zeros((SUBLANES, e), jnp.float32)

    x = x_ref[0]
    h = _modulated_input(x, mod_ref, g_ref).astype(jnp.bfloat16)

    for j in range(e // CONV_COLS):
        c0 = j * CONV_COLS
        cols = slice(c0, c0 + CONV_COLS)
        b_gate = _bdot(h, win_ref[:, c0:c0 + CONV_COLS])
        c_gate = _bdot(h, win_ref[:, e + c0:e + c0 + CONV_COLS])
        xin = _bdot(h, win_ref[:, 2 * e + c0:2 * e + c0 + CONV_COLS])
        z = _bdot(h, win_ref[:, 3 * e + c0:3 * e + c0 + CONV_COLS])
        cx = c_gate * xin
        cx_ref[SUBLANES:SUBLANES + ts, cols] = cx
        conv = cb_ref[:, cols] + cw_ref[2:3, cols] * cx
        conv = conv + cw_ref[0:1, cols] * cx_ref[SUBLANES - 2:SUBLANES - 2 + ts, cols]
        conv = conv + cw_ref[1:2, cols] * cx_ref[SUBLANES - 1:SUBLANES - 1 + ts, cols]
        y_ref[:, cols] = (_silu(z) * b_gate * conv).astype(jnp.bfloat16)

    cx_ref[0:SUBLANES, :] = cx_ref[ts:ts + SUBLANES, :]
    gate = mod_ref[0, 2:3, :]
    o_ref[0] = x + gate * _bdot(y_ref[...], wout_ref[...])


def _gmlp_layer_kernel(x_ref, mod_ref, g_ref, win_ref, lng_ref, lnb_ref, ws_ref, bs_ref,
                       wout_ref, fg_ref, o_ref, v_ref, y_ref):
    ts = x_ref.shape[1]
    e = D_INNER
    gw = GROUP_WIDTH

    x = x_ref[0]
    h = _modulated_input(x, mod_ref, g_ref).astype(jnp.bfloat16)

    row_sum = jnp.zeros((ts, 1), jnp.float32)
    for g in range(GMLP_GROUPS):
        c0 = g * gw
        v = _gelu_exact(_bdot(h, win_ref[:, e + c0:e + c0 + gw]))
        v_ref[:, c0:c0 + gw] = v
        row_sum = row_sum + jnp.sum(v, axis=-1, keepdims=True)
    mu = row_sum * (1.0 / e)
    sq_sum = jnp.zeros((ts, 1), jnp.float32)
    for g in range(GMLP_GROUPS):
        c0 = g * gw
        dv = v_ref[:, c0:c0 + gw] - mu
        sq_sum = sq_sum + jnp.sum(dv * dv, axis=-1, keepdims=True)
    rstd = lax.rsqrt(sq_sum * (1.0 / e) + LN_EPS)

    row = lax.broadcasted_iota(jnp.int32, (CHUNK, CHUNK), 0)
    col = lax.broadcasted_iota(jnp.int32, (CHUNK, CHUNK), 1)
    causal = col <= row

    for g in range(GMLP_GROUPS):
        c0 = g * gw
        cols = slice(c0, c0 + gw)
        u = _gelu_exact(_bdot(h, win_ref[:, c0:c0 + gw]))
        z = _bdot(h, win_ref[:, 2 * e + c0:2 * e + c0 + gw])
        vn = ((v_ref[:, cols] - mu) * rstd * lng_ref[:, cols] + lnb_ref[:, cols]).astype(jnp.bfloat16)
        w_g = jnp.where(causal, ws_ref[g], 0.0).astype(jnp.bfloat16)
        b_g = bs_ref[g]
        mixed = jnp.concatenate(
            [_bdot(w_g, vn[n * CHUNK:(n + 1) * CHUNK, :]) + b_g for n in range(ts // CHUNK)], axis=0)
        y_ref[:, cols] = (_silu(z) * (u * mixed)).astype(jnp.bfloat16)

    gate = mod_ref[0, 2:3, :]
    x_out = x + gate * _bdot(y_ref[...], wout_ref[...])
    o_ref[0] = _rms_norm(x_out, fg_ref[...])


def _resident(shape):
    return pl.BlockSpec(shape, lambda b, t: (0,) * len(shape), pipeline_mode=pl.Buffered(1))


def _layer_call(body, x, mod, params, scratch_shapes, name):
    batch, seq, d = x.shape
    tile = pl.BlockSpec((1, SEQ_TILE, d), lambda b, t: (b, t, 0))
    return pl.pallas_call(
        body,
        grid=(batch, seq // SEQ_TILE),
        in_specs=[tile, pl.BlockSpec((1, 3, d), lambda b, t: (b, 0, 0))]
                 + [_resident(p.shape) for p in params],
        out_specs=tile,
        out_shape=jax.ShapeDtypeStruct(x.shape, x.dtype),
        scratch_shapes=scratch_shapes,
        compiler_params=pltpu.CompilerParams(
            dimension_semantics=("arbitrary", "arbitrary"),
            vmem_limit_bytes=VMEM_LIMIT_BYTES),
        name=name,
    )(x, mod, *params)


def kernel(x, c, mod_w, mod_b, norm_g, a_w_in, a_conv_w, a_conv_b, a_w_out,
           b_w_in, b_ln_g, b_ln_b, b_w_s, b_b_s, b_w_out, final_g):
    batch, seq, d = x.shape
    e = D_INNER
    assert seq % SEQ_TILE == 0 and SEQ_TILE % CHUNK == 0 and d == D_MODEL
    assert mod_w.shape[0] == 2 and a_w_in.shape[0] == 1 and b_w_in.shape[0] == 1

    c_pad = jnp.pad(c, ((0, SUBLANES - batch), (0, 0)))
    mod = _adaln_mod(c_pad, mod_w, mod_b).reshape(2, SUBLANES, 3, d)

    bf16 = jnp.bfloat16
    x1 = _layer_call(
        _conv_layer_kernel, x, mod[0],
        [norm_g[0].reshape(1, d), a_w_in[0].astype(bf16), a_conv_w[0],
         a_conv_b[0].reshape(1, e), a_w_out[0].astype(bf16)],
        [pltpu.VMEM((SEQ_TILE + SUBLANES, e), jnp.float32),
         pltpu.VMEM((SEQ_TILE, e), bf16)],
        "conv_layer")
    return _layer_call(
        _gmlp_layer_kernel, x1, mod[1],
        [norm_g[1].reshape(1, d), b_w_in[0].astype(bf16), b_ln_g[0].reshape(1, e),
         b_ln_b[0].reshape(1, e), b_w_s[0], b_b_s[0].reshape(GMLP_GROUPS, CHUNK, 1),
         b_w_out[0].astype(bf16), final_g.reshape(1, d)],
        [pltpu.VMEM((SEQ_TILE, e), jnp.float32),
         pltpu.VMEM((SEQ_TILE, e), bf16)],
        "gmlp_layer")
```

```python
import functools
import math

import jax
import jax.numpy as jnp
from jax import lax
from jax.experimental import pallas as pl
from jax.experimental.pallas import tpu as pltpu

D_MODEL = 1024
D_INNER = 2048
CHUNK = 128
GMLP_GROUPS = 8
GROUP_WIDTH = D_INNER // GMLP_GROUPS
RMS_EPS = 1e-6
LN_EPS = 1e-5

SUBLANES = 8
LANES = 128
SEQ_TILE = 512
CONV_COLS = 512
OUT_COLS = 512
VMEM_LIMIT_BYTES = 56 * 1024 * 1024

_SQRT_HALF = math.sqrt(0.5)


def _silu(v):
    return v * jax.nn.sigmoid(v)


def _gelu_exact(v):
    return 0.5 * v * (1.0 + lax.erf(v * _SQRT_HALF))


def _rms_norm(x, g):
    ms = jnp.mean(x * x, axis=-1, keepdims=True)
    return (x * lax.rsqrt(ms + RMS_EPS)) * g


def _bdot(a, b):
    return jnp.dot(a, b, preferred_element_type=jnp.float32)


def _mod_kernel(c_ref, w_ref, b_ref, o_ref):
    c_act = _silu(c_ref[...])
    o_ref[0, 0] = _bdot(c_act, w_ref[0]) + b_ref[0, 0]


def _adaln_mod(c, mod_w, mod_b):
    depth, d, n = mod_w.shape
    batch = c.shape[0]
    return pl.pallas_call(
        _mod_kernel,
        grid=(depth, n // d),
        in_specs=[
            pl.BlockSpec((batch, d), lambda i, j: (0, 0)),
            pl.BlockSpec((1, d, d), lambda i, j: (i, 0, j)),
            pl.BlockSpec((1, 1, 1, d), lambda i, j: (i, j, 0, 0)),
        ],
        out_specs=pl.BlockSpec((1, 1, batch, d), lambda i, j: (i, j, 0, 0)),
        out_shape=jax.ShapeDtypeStruct((depth, n // d, batch, d), jnp.float32),
        compiler_params=pltpu.CompilerParams(
            dimension_semantics=("arbitrary", "arbitrary"),
            vmem_limit_bytes=VMEM_LIMIT_BYTES),
        name="adaln_mod",
    )(c, mod_w, mod_b.reshape(depth, n // d, 1, d))


def _mod_row(mod_ref, which, b):
    return mod_ref[0, which, pl.ds(b, 1), :]


def _modulated(x_rows, mod_ref, g_ref, b):
    return _rms_norm(x_rows, g_ref[0]) * (1.0 + _mod_row(mod_ref, 1, b)) + _mod_row(mod_ref, 0, b)


def _residual_out(x_ref, mod_ref, y_ref, wout_ref, b):
    n_blocks, _, width = wout_ref.shape
    blocks = []
    for k in range(n_blocks):
        cols = slice(k * width, (k + 1) * width)
        blocks.append(x_ref[0, :, cols] + _mod_row(mod_ref, 2, b)[:, cols] * _bdot(y_ref[...], wout_ref[k]))
    return blocks


def _or_tree(parts):
    while len(parts) > 1:
        parts = [parts[k] | parts[k + 1] for k in range(0, len(parts), 2)]
    return parts[0]


def _zero_depending_on(a):
    rows, cols = a.shape
    u = lax.bitcast_convert_type(a, jnp.uint32)
    u = _or_tree([u[r:r + SUBLANES, :] for r in range(0, rows, SUBLANES)])
    u = _or_tree([u[:, c:c + LANES] for c in range(0, cols, LANES)])
    half = jnp.uint32(16)
    return lax.shift_right_logical(lax.shift_right_logical(u, half), half)[0:1, :]


def _tied(row, zero):
    n = row.shape[1]
    bits = lax.bitcast_convert_type(row, jnp.uint32) + jnp.tile(zero, (1, n // LANES))
    return lax.bitcast_convert_type(bits, jnp.float32)


def _next_h_parts(xn_ref, mod_ref, g_ref, tiles_per_seq, n_parts):
    b_next = jnp.minimum(pl.program_id(0) + 1, pl.num_programs(0) - 1) // tiles_per_seq
    rows = xn_ref.shape[1] // n_parts
    parts = []
    for q in range(n_parts):
        hq = _modulated(xn_ref[0, q * rows:(q + 1) * rows, :], mod_ref, g_ref, b_next)
        parts.append((hq.astype(jnp.bfloat16), _zero_depending_on(hq)))
    return parts


def _store_next_h(h_ref, parts):
    rows = parts[0][0].shape[0]
    for q, (hq, _) in enumerate(parts):
        h_ref[q * rows:(q + 1) * rows, :] = hq


def _conv_layer_kernel(tiles_per_seq, x_ref, xn_ref, mod_ref, g_ref, win_ref, cw_ref, cb_ref, wout_ref,
                       o_ref, h_ref, cx_ref, y_ref):
    ts = x_ref.shape[1]
    e = D_INNER
    i = pl.program_id(0)
    b = i // tiles_per_seq

    @pl.when(i == 0)
    def _():
        h_ref[...] = _modulated(x_ref[0], mod_ref, g_ref, 0).astype(jnp.bfloat16)

    @pl.when(lax.rem(i, tiles_per_seq) == 0)
    def _():
        cx_ref[0:SUBLANES, :] = jnp.zeros((SUBLANES, e), jnp.float32)

    n_chunks = e // CONV_COLS
    next_h = _next_h_parts(xn_ref, mod_ref, g_ref, tiles_per_seq, n_chunks)
    for j in range(n_chunks):
        cols = slice(j * CONV_COLS, (j + 1) * CONV_COLS)
        b_gate = _bdot(h_ref[...], win_ref[j])
        c_gate = _bdot(h_ref[...], win_ref[n_chunks + j])
        xin = _bdot(h_ref[...], win_ref[2 * n_chunks + j])
        z = _bdot(h_ref[...], win_ref[3 * n_chunks + j])
        cx = c_gate * xin
        cx_ref[SUBLANES:SUBLANES + ts, cols] = cx
        conv = _tied(cb_ref[:, cols], next_h[j][1]) + cw_ref[2:3, cols] * cx
        conv = conv + cw_ref[0:1, cols] * cx_ref[SUBLANES - 2:SUBLANES - 2 + ts, cols]
        conv = conv + cw_ref[1:2, cols] * cx_ref[SUBLANES - 1:SUBLANES - 1 + ts, cols]
        y_ref[:, cols] = (_silu(z) * b_gate * conv).astype(jnp.bfloat16)

    cx_ref[0:SUBLANES, :] = cx_ref[ts:ts + SUBLANES, :]
    _store_next_h(h_ref, next_h)
    width = wout_ref.shape[2]
    for k, blk in enumerate(_residual_out(x_ref, mod_ref, y_ref, wout_ref, b)):
        o_ref[0, :, k * width:(k + 1) * width] = blk


def _gmlp_layer_kernel(tiles_per_seq, x_ref, mod_ref, g_ref, win_ref, lng_ref, lnb_ref, ws_ref,
                       bs_ref, wout_ref, fg_ref, o_ref, h_ref, v_ref, y_ref):
    ts = x_ref.shape[1]
    e = D_INNER
    gw = GROUP_WIDTH
    b = pl.program_id(0) // tiles_per_seq

    h_ref[...] = _modulated(x_ref[0], mod_ref, g_ref, b).astype(jnp.bfloat16)

    d_sum = jnp.zeros((ts, 1), jnp.float32)
    d_sq = jnp.zeros((ts, 1), jnp.float32)
    for g in range(GMLP_GROUPS):
        v = _gelu_exact(_bdot(h_ref[...], win_ref[GMLP_GROUPS + g]))
        v_ref[:, g * gw:(g + 1) * gw] = v
        if g == 0:
            pivot = jnp.mean(v, axis=-1, keepdims=True)
        d = v - pivot
        d_sum = d_sum + jnp.sum(d, axis=-1, keepdims=True)
        d_sq = d_sq + jnp.sum(d * d, axis=-1, keepdims=True)
    d_mean = d_sum * (1.0 / e)
    mu = pivot + d_mean
    rstd = lax.rsqrt(d_sq * (1.0 / e) - d_mean * d_mean + LN_EPS)

    row = lax.broadcasted_iota(jnp.int32, (CHUNK, CHUNK), 0)
    col = lax.broadcasted_iota(jnp.int32, (CHUNK, CHUNK), 1)
    causal = col <= row

    for g in range(GMLP_GROUPS):
        cols = slice(g * gw, (g + 1) * gw)
        u = _gelu_exact(_bdot(h_ref[...], win_ref[g]))
        z = _bdot(h_ref[...], win_ref[2 * GMLP_GROUPS + g])
        vn = ((v_ref[:, cols] - mu) * rstd * lng_ref[:, cols] + lnb_ref[:, cols]).astype(jnp.bfloat16)
        w_g = jnp.where(causal, ws_ref[g], 0.0).astype(jnp.bfloat16)
        b_g = bs_ref[g]
        mixed = jnp.concatenate(
            [_bdot(w_g, vn[n * CHUNK:(n + 1) * CHUNK, :]) + b_g for n in range(ts // CHUNK)], axis=0)
        y_ref[:, cols] = (_silu(z) * (u * mixed)).astype(jnp.bfloat16)

    x_out = jnp.concatenate(_residual_out(x_ref, mod_ref, y_ref, wout_ref, b), axis=1)
    o_ref[0] = _rms_norm(x_out, fg_ref[...])


def _split_cols(w, width):
    k, n = w.shape
    return w.astype(jnp.bfloat16).reshape(k, n // width, width).transpose(1, 0, 2)


def _resident(shape, lead=None):
    if lead is None:
        return pl.BlockSpec(tuple(shape), lambda i: (0,) * len(shape), pipeline_mode=pl.Buffered(1))
    return pl.BlockSpec((1,) + tuple(shape[1:]), lambda i: (lead,) + (0,) * (len(shape) - 1),
                        pipeline_mode=pl.Buffered(1))


def _layer_call(body, layer, x, mod, norm_g, params, scratch_shapes, name, stage_next_tile):
    batch, seq, d = x.shape
    tiles_per_seq = seq // SEQ_TILE
    steps = batch * tiles_per_seq

    def tile_index(i):
        return (i // tiles_per_seq, lax.rem(i, tiles_per_seq), 0)

    tile = pl.BlockSpec((1, SEQ_TILE, d), tile_index)
    x_specs, x_args = [tile], [x]
    if stage_next_tile:
        x_specs.append(pl.BlockSpec((1, SEQ_TILE, d), lambda i: tile_index(jnp.minimum(i + 1, steps - 1))))
        x_args.append(x)
    return pl.pallas_call(
        functools.partial(body, tiles_per_seq),
        grid=(steps,),
        in_specs=x_specs + [_resident(mod.shape, layer), _resident(norm_g.shape, layer)]
                 + [_resident(p.shape) for p in params],
        out_specs=tile,
        out_shape=jax.ShapeDtypeStruct(x.shape, x.dtype),
        scratch_shapes=[pltpu.VMEM((SEQ_TILE, d), jnp.bfloat16)] + scratch_shapes,
        compiler_params=pltpu.CompilerParams(
            dimension_semantics=("arbitrary",),
            vmem_limit_bytes=VMEM_LIMIT_BYTES),
        name=name,
    )(*x_args, mod, norm_g, *params)


def kernel(x, c, mod_w, mod_b, norm_g, a_w_in, a_conv_w, a_conv_b, a_w_out,
           b_w_in, b_ln_g, b_ln_b, b_w_s, b_b_s, b_w_out, final_g):
    batch, seq, d = x.shape
    e = D_INNER
    assert seq % SEQ_TILE == 0 and SEQ_TILE % CHUNK == 0 and d == D_MODEL
    assert mod_w.shape[0] == 2 and a_w_in.shape[0] == 1 and b_w_in.shape[0] == 1

    mod = _adaln_mod(c, mod_w, mod_b)
    g3 = norm_g.reshape(2, 1, d)
    bf16 = jnp.bfloat16
    x1 = _layer_call(
        _conv_layer_kernel, 0, x, mod, g3,
        [_split_cols(a_w_in[0], CONV_COLS), a_conv_w[0], a_conv_b, _split_cols(a_w_out[0], OUT_COLS)],
        [pltpu.VMEM((SEQ_TILE + SUBLANES, e), jnp.float32),
         pltpu.VMEM((SEQ_TILE, e), bf16)],
        "conv_layer", stage_next_tile=True)
    return _layer_call(
        _gmlp_layer_kernel, 1, x1, mod, g3,
        [_split_cols(b_w_in[0], GROUP_WIDTH), b_ln_g, b_ln_b, b_w_s[0],
         b_b_s.reshape(GMLP_GROUPS, CHUNK, 1), _split_cols(b_w_out[0], OUT_COLS), final_g.reshape(1, d)],
        [pltpu.VMEM((SEQ_TILE, e), jnp.float32),
         pltpu.VMEM((SEQ_TILE, e), bf16)],
        "gmlp_layer", stage_next_tile=False)
```

```python
import functools
import math

import jax
import jax.numpy as jnp
from jax import lax
from jax.experimental import pallas as pl
from jax.experimental.pallas import tpu as pltpu

D_MODEL = 1024
D_INNER = 2048
CHUNK = 128
GMLP_GROUPS = 8
GROUP_WIDTH = D_INNER // GMLP_GROUPS
RMS_EPS = 1e-6
LN_EPS = 1e-5

SUBLANES = 8
LANES = 128
SEQ_TILE = 512
CONV_COLS = 512
OUT_COLS = 512
PREP_COLS = 512
PREP_ROWS = 512
VMEM_LIMIT_BYTES = 56 * 1024 * 1024

_SQRT_HALF = math.sqrt(0.5)


def _silu(v):
    return v * jax.nn.sigmoid(v)


def _gelu_exact(v):
    return 0.5 * v * (1.0 + lax.erf(v * _SQRT_HALF))


def _rms_norm(x, g):
    ms = jnp.mean(x * x, axis=-1, keepdims=True)
    return (x * lax.rsqrt(ms + RMS_EPS)) * g


def _bdot(a, b):
    return jnp.dot(a, b, preferred_element_type=jnp.float32)


def _mod_kernel(c_ref, w_ref, b_ref, o_ref):
    c_act = _silu(c_ref[...])
    o_ref[0, 0] = _bdot(c_act, w_ref[0]) + b_ref[0, 0]


def _adaln_mod(c, mod_w, mod_b):
    depth, d, n = mod_w.shape
    batch = c.shape[0]
    return pl.pallas_call(
        _mod_kernel,
        grid=(depth, n // d),
        in_specs=[
            pl.BlockSpec((batch, d), lambda i, j: (0, 0)),
            pl.BlockSpec((1, d, d), lambda i, j: (i, 0, j)),
            pl.BlockSpec((1, 1, 1, d), lambda i, j: (i, j, 0, 0)),
        ],
        out_specs=pl.BlockSpec((1, 1, batch, d), lambda i, j: (i, j, 0, 0)),
        out_shape=jax.ShapeDtypeStruct((depth, n // d, batch, d), jnp.float32),
        compiler_params=pltpu.CompilerParams(
            dimension_semantics=("arbitrary", "arbitrary"),
            vmem_limit_bytes=VMEM_LIMIT_BYTES),
        name="adaln_mod",
    )(c, mod_w, mod_b.reshape(depth, n // d, 1, d))


def _mod_row(mod_ref, which, b):
    return mod_ref[0, which, pl.ds(b, 1), :]


def _modulated(x_rows, mod_ref, g_ref, b):
    return _rms_norm(x_rows, g_ref[0]) * (1.0 + _mod_row(mod_ref, 1, b)) + _mod_row(mod_ref, 0, b)


def _residual_out(x_ref, mod_ref, y_ref, wout_ref, b):
    n_blocks, _, width = wout_ref.shape
    blocks = []
    for k in range(n_blocks):
        cols = slice(k * width, (k + 1) * width)
        blocks.append(x_ref[0, :, cols] + _mod_row(mod_ref, 2, b)[:, cols] * _bdot(y_ref[...], wout_ref[k]))
    return blocks


def _or_tree(parts):
    while len(parts) > 1:
        parts = [parts[k] | parts[k + 1] for k in range(0, len(parts), 2)]
    return parts[0]


def _zero_depending_on(a):
    rows, cols = a.shape
    u = lax.bitcast_convert_type(a, jnp.uint32)
    u = _or_tree([u[r:r + SUBLANES, :] for r in range(0, rows, SUBLANES)])
    u = _or_tree([u[:, c:c + LANES] for c in range(0, cols, LANES)])
    half = jnp.uint32(16)
    return lax.shift_right_logical(lax.shift_right_logical(u, half), half)[0:1, :]


def _tied(row, zero):
    n = row.shape[1]
    bits = lax.bitcast_convert_type(row, jnp.uint32) + jnp.tile(zero, (1, n // LANES))
    return lax.bitcast_convert_type(bits, jnp.float32)


def _next_h_parts(xn_ref, mod_ref, g_ref, b_next, n_parts):
    rows = xn_ref.shape[1] // n_parts
    parts = []
    for q in range(n_parts):
        hq = _modulated(xn_ref[0, q * rows:(q + 1) * rows, :], mod_ref, g_ref, b_next)
        parts.append((hq.astype(jnp.bfloat16), _zero_depending_on(hq)))
    return parts


def _store_next_h(h_ref, parts):
    rows = parts[0][0].shape[0]
    for q, (hq, _) in enumerate(parts):
        h_ref[q * rows:(q + 1) * rows, :] = hq


def _prepare_weights(win_blk_ref, wout_blk_ref, win_ref, wout_ref):
    i = pl.program_id(0)
    n_blocks, _, width = win_ref.shape
    per_step = win_blk_ref.shape[1] // width
    n_in = n_blocks // per_step
    out_blocks, k_out, out_width = wout_ref.shape
    rows = wout_blk_ref.shape[0]
    n_out = k_out // rows

    @pl.when(i < n_in)
    def _():
        for p in range(per_step):
            win_ref[i * per_step + p] = win_blk_ref[:, p * width:(p + 1) * width].astype(jnp.bfloat16)

    @pl.when(jnp.logical_and(i >= n_in, i < n_in + n_out))
    def _():
        r0 = pl.multiple_of((i - n_in) * rows, rows)
        for k in range(out_blocks):
            wout_ref[k, pl.ds(r0, rows), :] = (
                wout_blk_ref[:, k * out_width:(k + 1) * out_width].astype(jnp.bfloat16))

    return n_in + n_out


def _conv_layer_kernel(tiles_per_seq, n_tiles, x_ref, xn_ref, mod_ref, g_ref, win_blk_ref, wout_blk_ref,
                       cw_ref, cb_ref, o_ref, win_ref, wout_ref, h_ref, cx_ref, y_ref):
    n_prep = _prepare_weights(win_blk_ref, wout_blk_ref, win_ref, wout_ref)

    @pl.when(pl.program_id(0) >= n_prep)
    def _():
        _conv_tile(tiles_per_seq, n_tiles, pl.program_id(0) - n_prep, x_ref, xn_ref, mod_ref, g_ref, win_ref,
                   cw_ref, cb_ref, wout_ref, o_ref, h_ref, cx_ref, y_ref)


def _conv_tile(tiles_per_seq, n_tiles, i, x_ref, xn_ref, mod_ref, g_ref, win_ref, cw_ref, cb_ref, wout_ref,
               o_ref, h_ref, cx_ref, y_ref):
    ts = x_ref.shape[1]
    e = D_INNER
    b = i // tiles_per_seq
    b_next = jnp.minimum(i + 1, n_tiles - 1) // tiles_per_seq

    @pl.when(i == 0)
    def _():
        h_ref[...] = _modulated(x_ref[0], mod_ref, g_ref, 0).astype(jnp.bfloat16)

    @pl.when(lax.rem(i, tiles_per_seq) == 0)
    def _():
        cx_ref[0:SUBLANES, :] = jnp.zeros((SUBLANES, e), jnp.float32)

    n_chunks = e // CONV_COLS
    next_h = _next_h_parts(xn_ref, mod_ref, g_ref, b_next, n_chunks)
    for j in range(n_chunks):
        cols = slice(j * CONV_COLS, (j + 1) * CONV_COLS)
        b_gate = _bdot(h_ref[...], win_ref[j])
        c_gate = _bdot(h_ref[...], win_ref[n_chunks + j])
        xin = _bdot(h_ref[...], win_ref[2 * n_chunks + j])
        z = _bdot(h_ref[...], win_ref[3 * n_chunks + j])
        cx = c_gate * xin
        cx_ref[SUBLANES:SUBLANES + ts, cols] = cx
        conv = _tied(cb_ref[:, cols], next_h[j][1]) + cw_ref[2:3, cols] * cx
        conv = conv + cw_ref[0:1, cols] * cx_ref[SUBLANES - 2:SUBLANES - 2 + ts, cols]
        conv = conv + cw_ref[1:2, cols] * cx_ref[SUBLANES - 1:SUBLANES - 1 + ts, cols]
        y_ref[:, cols] = (_silu(z) * b_gate * conv).astype(jnp.bfloat16)

    cx_ref[0:SUBLANES, :] = cx_ref[ts:ts + SUBLANES, :]
    _store_next_h(h_ref, next_h)
    width = wout_ref.shape[2]
    for k, blk in enumerate(_residual_out(x_ref, mod_ref, y_ref, wout_ref, b)):
        o_ref[0, :, k * width:(k + 1) * width] = blk


def _gmlp_layer_kernel(tiles_per_seq, n_tiles, x_ref, mod_ref, g_ref, win_blk_ref, wout_blk_ref,
                       lng_ref, lnb_ref, ws_ref, bs_ref, fg_ref, o_ref, win_ref, wout_ref, h_ref, v_ref, y_ref):
    n_prep = _prepare_weights(win_blk_ref, wout_blk_ref, win_ref, wout_ref)

    @pl.when(pl.program_id(0) >= n_prep)
    def _():
        b = (pl.program_id(0) - n_prep) // tiles_per_seq
        _gmlp_tile(b, x_ref, mod_ref, g_ref, win_ref, lng_ref, lnb_ref, ws_ref, bs_ref, wout_ref, fg_ref,
                   o_ref, h_ref, v_ref, y_ref)


def _gmlp_tile(b, x_ref, mod_ref, g_ref, win_ref, lng_ref, lnb_ref, ws_ref, bs_ref, wout_ref, fg_ref,
               o_ref, h_ref, v_ref, y_ref):
    ts = x_ref.shape[1]
    e = D_INNER
    gw = GROUP_WIDTH

    h_ref[...] = _modulated(x_ref[0], mod_ref, g_ref, b).astype(jnp.bfloat16)

    d_sum = jnp.zeros((ts, 1), jnp.float32)
    d_sq = jnp.zeros((ts, 1), jnp.float32)
    for g in range(GMLP_GROUPS):
        v = _gelu_exact(_bdot(h_ref[...], win_ref[GMLP_GROUPS + g]))
        v_ref[:, g * gw:(g + 1) * gw] = v
        if g == 0:
            pivot = jnp.mean(v, axis=-1, keepdims=True)
        d = v - pivot
        d_sum = d_sum + jnp.sum(d, axis=-1, keepdims=True)
        d_sq = d_sq + jnp.sum(d * d, axis=-1, keepdims=True)
    d_mean = d_sum * (1.0 / e)
    mu = pivot + d_mean
    rstd = lax.rsqrt(d_sq * (1.0 / e) - d_mean * d_mean + LN_EPS)

    row = lax.broadcasted_iota(jnp.int32, (CHUNK, CHUNK), 0)
    col = lax.broadcasted_iota(jnp.int32, (CHUNK, CHUNK), 1)
    causal = col <= row

    for g in range(GMLP_GROUPS):
        cols = slice(g * gw, (g + 1) * gw)
        u = _gelu_exact(_bdot(h_ref[...], win_ref[g]))
        z = _bdot(h_ref[...], win_ref[2 * GMLP_GROUPS + g])
        vn = ((v_ref[:, cols] - mu) * rstd * lng_ref[:, cols] + lnb_ref[:, cols]).astype(jnp.bfloat16)
        w_g = jnp.where(causal, ws_ref[g], 0.0).astype(jnp.bfloat16)
        b_g = bs_ref[g]
        mixed = jnp.concatenate(
            [_bdot(w_g, vn[n * CHUNK:(n + 1) * CHUNK, :]) + b_g for n in range(ts // CHUNK)], axis=0)
        y_ref[:, cols] = (_silu(z) * (u * mixed)).astype(jnp.bfloat16)

    x_out = jnp.concatenate(_residual_out(x_ref, mod_ref, y_ref, wout_ref, b), axis=1)
    o_ref[0] = _rms_norm(x_out, fg_ref[...])


def _resident(shape, lead=None):
    if lead is None:
        return pl.BlockSpec(tuple(shape), lambda i: (0,) * len(shape), pipeline_mode=pl.Buffered(1))
    return pl.BlockSpec((1,) + tuple(shape[1:]), lambda i: (lead,) + (0,) * (len(shape) - 1),
                        pipeline_mode=pl.Buffered(1))


def _layer_call(body, layer, x, mod, norm_g, w_in, w_out, in_width, params, scratch_shapes, name,
                stage_next_tile):
    batch, seq, d = x.shape
    tiles_per_seq = seq // SEQ_TILE
    n_tiles = batch * tiles_per_seq
    n_in = w_in.shape[1] // PREP_COLS
    n_out = w_out.shape[0] // PREP_ROWS
    n_prep = n_in + n_out

    def tile_index(t):
        return (t // tiles_per_seq, lax.rem(t, tiles_per_seq), 0)

    def tile_of_step(i):
        return jnp.maximum(i - n_prep, 0)

    tile = pl.BlockSpec((1, SEQ_TILE, d), lambda i: tile_index(tile_of_step(i)))
    x_specs, x_args = [tile], [x]
    if stage_next_tile:
        x_specs.append(pl.BlockSpec(
            (1, SEQ_TILE, d), lambda i: tile_index(jnp.minimum(tile_of_step(i) + 1, n_tiles - 1))))
        x_args.append(x)
    weight_specs = [
        pl.BlockSpec((w_in.shape[0], PREP_COLS), lambda i: (0, jnp.minimum(i, n_in - 1))),
        pl.BlockSpec((PREP_ROWS, w_out.shape[1]), lambda i: (jnp.clip(i - n_in, 0, n_out - 1), 0)),
    ]
    weight_scratch = [
        pltpu.VMEM((w_in.shape[1] // in_width, w_in.shape[0], in_width), jnp.bfloat16),
        pltpu.VMEM((w_out.shape[1] // OUT_COLS, w_out.shape[0], OUT_COLS), jnp.bfloat16),
        pltpu.VMEM((SEQ_TILE, d), jnp.bfloat16),
    ]
    return pl.pallas_call(
        functools.partial(body, tiles_per_seq, n_tiles),
        grid=(n_prep + n_tiles,),
        in_specs=x_specs + [_resident(mod.shape, layer), _resident(norm_g.shape, layer)] + weight_specs
                 + [_resident(p.shape) for p in params],
        out_specs=tile,
        out_shape=jax.ShapeDtypeStruct(x.shape, x.dtype),
        scratch_shapes=weight_scratch + scratch_shapes,
        compiler_params=pltpu.CompilerParams(
            dimension_semantics=("arbitrary",),
            vmem_limit_bytes=VMEM_LIMIT_BYTES),
        name=name,
    )(*x_args, mod, norm_g, w_in, w_out, *params)


def kernel(x, c, mod_w, mod_b, norm_g, a_w_in, a_conv_w, a_conv_b, a_w_out,
           b_w_in, b_ln_g, b_ln_b, b_w_s, b_b_s, b_w_out, final_g):
    batch, seq, d = x.shape
    e = D_INNER
    assert seq % SEQ_TILE == 0 and SEQ_TILE % CHUNK == 0 and d == D_MODEL
    assert mod_w.shape[0] == 2 and a_w_in.shape[0] == 1 and b_w_in.shape[0] == 1

    mod = _adaln_mod(c, mod_w, mod_b)
    g3 = norm_g.reshape(2, 1, d)
    bf16 = jnp.bfloat16
    x1 = _layer_call(
        _conv_layer_kernel, 0, x, mod, g3, a_w_in[0], a_w_out[0], CONV_COLS,
        [a_conv_w[0], a_conv_b],
        [pltpu.VMEM((SEQ_TILE + SUBLANES, e), jnp.float32),
         pltpu.VMEM((SEQ_TILE, e), bf16)],
        "conv_layer", stage_next_tile=True)
    return _layer_call(
        _gmlp_layer_kernel, 1, x1, mod, g3, b_w_in[0], b_w_out[0], GROUP_WIDTH,
        [b_ln_g, b_ln_b, b_w_s[0], b_b_s.reshape(GMLP_GROUPS, CHUNK, 1), final_g.reshape(1, d)],
        [pltpu.VMEM((SEQ_TILE, e), jnp.float32),
         pltpu.VMEM((SEQ_TILE, e), bf16)],
        "gmlp_layer", stage_next_tile=False)
```

```python
import functools
import math

import jax
import jax.numpy as jnp
from jax import lax
from jax.experimental import pallas as pl
from jax.experimental.pallas import tpu as pltpu

D_MODEL = 1024
D_INNER = 2048
CHUNK = 128
GMLP_GROUPS = 8
GROUP_WIDTH = D_INNER // GMLP_GROUPS
RMS_EPS = 1e-6
LN_EPS = 1e-5

SUBLANES = 8
LANES = 128
SEQ_TILE = 512
HEAD_ROW_BLOCKS = 4
TAIL_ROW_BLOCKS = 2
CONV_COLS = 512
OUT_COLS = 512
PREP_COLS = 1024
PREP_ROWS = 1024
VMEM_LIMIT_BYTES = 60 * 1024 * 1024

_SQRT_HALF = math.sqrt(0.5)


def _silu(v):
    return v * jax.nn.sigmoid(v)


def _gelu_exact(v):
    return 0.5 * v * (1.0 + lax.erf(v * _SQRT_HALF))


def _rms_norm(x, g):
    ms = jnp.mean(x * x, axis=-1, keepdims=True)
    return (x * lax.rsqrt(ms + RMS_EPS)) * g


def _bdot(a, b):
    return jnp.dot(a, b, preferred_element_type=jnp.float32)


def _mod_kernel(c_ref, w_ref, b_ref, o_ref):
    c_act = _silu(c_ref[...])
    o_ref[0, 0] = _bdot(c_act, w_ref[0]) + b_ref[pl.ds(pl.program_id(0), 1), :]


def _adaln_mod(c, mod_w, mod_b):
    depth, d, n = mod_w.shape
    batch = c.shape[0]
    return pl.pallas_call(
        _mod_kernel,
        grid=(depth, n // d),
        in_specs=[
            pl.BlockSpec((batch, d), lambda i, j: (0, 0)),
            pl.BlockSpec((1, d, d), lambda i, j: (i, 0, j)),
            pl.BlockSpec((depth, d), lambda i, j: (0, j)),
        ],
        out_specs=pl.BlockSpec((1, 1, batch, d), lambda i, j: (i, j, 0, 0)),
        out_shape=jax.ShapeDtypeStruct((depth, n // d, batch, d), jnp.float32),
        compiler_params=pltpu.CompilerParams(
            dimension_semantics=("arbitrary", "arbitrary"),
            vmem_limit_bytes=VMEM_LIMIT_BYTES),
        name="adaln_mod",
    )(c, mod_w, mod_b)


def _mod_row(mod_ref, which, b):
    return mod_ref[0, which, pl.ds(b, 1), :]


def _modulated(x_rows, mod_ref, g_row, b):
    return _rms_norm(x_rows, g_row) * (1.0 + _mod_row(mod_ref, 1, b)) + _mod_row(mod_ref, 0, b)


def _residual_out(x_ref, mod_ref, y_ref, wout_ref, b, rows=slice(None)):
    n_blocks, _, width = wout_ref.shape
    blocks = []
    for k in range(n_blocks):
        cols = slice(k * width, (k + 1) * width)
        blocks.append(x_ref[0, rows, cols] + _mod_row(mod_ref, 2, b)[:, cols] * _bdot(y_ref[rows, :], wout_ref[k]))
    return blocks


def _or_tree(parts):
    while len(parts) > 1:
        parts = [parts[k] | parts[k + 1] for k in range(0, len(parts), 2)]
    return parts[0]


def _zero_depending_on(a):
    rows, cols = a.shape
    u = lax.bitcast_convert_type(a, jnp.uint32)
    u = _or_tree([u[r:r + SUBLANES, :] for r in range(0, rows, SUBLANES)])
    u = _or_tree([u[:, c:c + LANES] for c in range(0, cols, LANES)])
    half = jnp.uint32(16)
    return lax.shift_right_logical(lax.shift_right_logical(u, half), half)[0:1, :]


def _tied(row, zero):
    n = row.shape[1]
    bits = lax.bitcast_convert_type(row, jnp.uint32) + jnp.tile(zero, (1, n // LANES))
    return lax.bitcast_convert_type(bits, jnp.float32)


def _next_h_parts(xn_ref, mod_ref, g_row, b_next, n_parts):
    rows = xn_ref.shape[1] // n_parts
    parts = []
    for q in range(n_parts):
        hq = _modulated(xn_ref[0, q * rows:(q + 1) * rows, :], mod_ref, g_row, b_next)
        parts.append((hq.astype(jnp.bfloat16), _zero_depending_on(hq)))
    return parts


def _store_next_h(h_ref, parts):
    rows = parts[0][0].shape[0]
    for q, (hq, _) in enumerate(parts):
        h_ref[q * rows:(q + 1) * rows, :] = hq


def _prepare_weights(win_blk_ref, wout_blk_ref, win_ref, wout_ref):
    i = pl.program_id(0)
    n_blocks, _, width = win_ref.shape
    per_step = win_blk_ref.shape[1] // width
    n_in = n_blocks // per_step
    out_blocks, k_out, out_width = wout_ref.shape
    rows = wout_blk_ref.shape[0]
    n_out = k_out // rows

    @pl.when(i < n_in)
    def _():
        for p in range(per_step):
            win_ref[i * per_step + p] = win_blk_ref[:, p * width:(p + 1) * width].astype(jnp.bfloat16)

    @pl.when(jnp.logical_and(i >= n_in, i < n_in + n_out))
    def _():
        r0 = pl.multiple_of((i - n_in) * rows, rows)
        for k in range(out_blocks):
            wout_ref[k, pl.ds(r0, rows), :] = (
                wout_blk_ref[:, k * out_width:(k + 1) * out_width].astype(jnp.bfloat16))

    return n_in + n_out


def _conv_layer_kernel(layer, tiles_per_seq, n_tiles, x_ref, xn_ref, mod_ref, g_ref, win_blk_ref, wout_blk_ref,
                       cw_ref, cb_ref, o_ref, win_ref, wout_ref, h_ref, cx_ref, y_ref):
    n_prep = _prepare_weights(win_blk_ref, wout_blk_ref, win_ref, wout_ref)

    @pl.when(pl.program_id(0) >= n_prep)
    def _():
        _conv_tile(tiles_per_seq, n_tiles, pl.program_id(0) - n_prep, x_ref, xn_ref, mod_ref,
                   g_ref[layer:layer + 1, :], win_ref, cw_ref, cb_ref, wout_ref, o_ref, h_ref, cx_ref, y_ref)


def _conv_tile(tiles_per_seq, n_tiles, i, x_ref, xn_ref, mod_ref, g_row, win_ref, cw_ref, cb_ref, wout_ref,
               o_ref, h_ref, cx_ref, y_ref):
    ts = x_ref.shape[1]
    e = D_INNER
    b = i // tiles_per_seq
    b_next = jnp.minimum(i + 1, n_tiles - 1) // tiles_per_seq

    @pl.when(i == 0)
    def _():
        h_ref[...] = _modulated(x_ref[0], mod_ref, g_row, 0).astype(jnp.bfloat16)

    @pl.when(lax.rem(i, tiles_per_seq) == 0)
    def _():
        cx_ref[0:SUBLANES, :] = jnp.zeros((SUBLANES, e), jnp.float32)

    n_chunks = e // CONV_COLS
    next_h = _next_h_parts(xn_ref, mod_ref, g_row, b_next, n_chunks)
    for j in range(n_chunks):
        cols = slice(j * CONV_COLS, (j + 1) * CONV_COLS)
        b_gate = _bdot(h_ref[...], win_ref[j])
        c_gate = _bdot(h_ref[...], win_ref[n_chunks + j])
        xin = _bdot(h_ref[...], win_ref[2 * n_chunks + j])
        z = _bdot(h_ref[...], win_ref[3 * n_chunks + j])
        cx = c_gate * xin
        cx_ref[SUBLANES:SUBLANES + ts, cols] = cx
        conv = _tied(cb_ref[:, cols], next_h[j][1]) + cw_ref[2:3, cols] * cx
        conv = conv + cw_ref[0:1, cols] * cx_ref[SUBLANES - 2:SUBLANES - 2 + ts, cols]
        conv = conv + cw_ref[1:2, cols] * cx_ref[SUBLANES - 1:SUBLANES - 1 + ts, cols]
        y_ref[:, cols] = (_silu(z) * b_gate * conv).astype(jnp.bfloat16)

    cx_ref[0:SUBLANES, :] = cx_ref[ts:ts + SUBLANES, :]
    _store_next_h(h_ref, next_h)
    width = wout_ref.shape[2]
    for k, blk in enumerate(_residual_out(x_ref, mod_ref, y_ref, wout_ref, b)):
        o_ref[0, :, k * width:(k + 1) * width] = blk


def _gmlp_layer_kernel(layer, tiles_per_seq, n_tiles, x_ref, mod_ref, g_ref, win_blk_ref, wout_blk_ref,
                       lng_ref, lnb_ref, ws_ref, bs_ref, fg_ref, o_ref, win_ref, wout_ref, h_ref, v_ref, y_ref):
    n_prep = _prepare_weights(win_blk_ref, wout_blk_ref, win_ref, wout_ref)

    @pl.when(pl.program_id(0) >= n_prep)
    def _():
        b = (pl.program_id(0) - n_prep) // tiles_per_seq
        _gmlp_tile(b, x_ref, mod_ref, g_ref[layer:layer + 1, :], win_ref, lng_ref, lnb_ref, ws_ref, bs_ref,
                   wout_ref, fg_ref, o_ref, h_ref, v_ref, y_ref)


def _gmlp_tile(b, x_ref, mod_ref, g_row, win_ref, lng_ref, lnb_ref, ws_ref, bs_ref, wout_ref, fg_ref,
               o_ref, h_ref, v_ref, y_ref):
    ts = x_ref.shape[1]
    e = D_INNER
    gw = GROUP_WIDTH

    h_ref[...] = _modulated(x_ref[0], mod_ref, g_row, b).astype(jnp.bfloat16)

    d_sum = jnp.zeros((ts, 1), jnp.float32)
    d_sq = jnp.zeros((ts, 1), jnp.float32)
    for g in range(GMLP_GROUPS):
        if g == 0:
            rb = ts // HEAD_ROW_BLOCKS
            v = jnp.concatenate([_gelu_exact(_bdot(h_ref[r * rb:(r + 1) * rb, :], win_ref[GMLP_GROUPS]))
                                 for r in range(HEAD_ROW_BLOCKS)], axis=0)
        else:
            v = _gelu_exact(_bdot(h_ref[...], win_ref[GMLP_GROUPS + g]))
        v_ref[:, g * gw:(g + 1) * gw] = v
        if g == 0:
            pivot = jnp.mean(v, axis=-1, keepdims=True)
        d = v - pivot
        d_sum = d_sum + jnp.sum(d, axis=-1, keepdims=True)
        d_sq = d_sq + jnp.sum(d * d, axis=-1, keepdims=True)
    d_mean = d_sum * (1.0 / e)
    mu = pivot + d_mean
    rstd = lax.rsqrt(d_sq * (1.0 / e) - d_mean * d_mean + LN_EPS)

    row = lax.broadcasted_iota(jnp.int32, (CHUNK, CHUNK), 0)
    col = lax.broadcasted_iota(jnp.int32, (CHUNK, CHUNK), 1)
    causal = col <= row
    diagonal = col == row

    for g in range(GMLP_GROUPS):
        cols = slice(g * gw, (g + 1) * gw)
        u = _gelu_exact(_bdot(h_ref[...], win_ref[g]))
        z = _bdot(h_ref[...], win_ref[2 * GMLP_GROUPS + g])
        vn = ((v_ref[:, cols] - mu) * rstd * lng_ref[:, cols] + lnb_ref[:, cols]).astype(jnp.bfloat16)
        w_g = jnp.where(causal, ws_ref[g], 0.0).astype(jnp.bfloat16)
        b_g = jnp.sum(jnp.where(diagonal, bs_ref[g:g + 1, :], 0.0), axis=1, keepdims=True)
        mixed = jnp.concatenate(
            [_bdot(w_g, vn[n * CHUNK:(n + 1) * CHUNK, :]) + b_g for n in range(ts // CHUNK)], axis=0)
        y_ref[:, cols] = (_silu(z) * (u * mixed)).astype(jnp.bfloat16)

    rb = ts // TAIL_ROW_BLOCKS
    for r in range(TAIL_ROW_BLOCKS):
        rows = slice(r * rb, (r + 1) * rb)
        x_out = jnp.concatenate(_residual_out(x_ref, mod_ref, y_ref, wout_ref, b, rows), axis=1)
        o_ref[0, rows, :] = _rms_norm(x_out, fg_ref[...])


def _resident(shape, lead=None):
    if lead is None:
        return pl.BlockSpec(tuple(shape), lambda i: (0,) * len(shape), pipeline_mode=pl.Buffered(1))
    return pl.BlockSpec((1,) + tuple(shape[1:]), lambda i: (lead,) + (0,) * (len(shape) - 1),
                        pipeline_mode=pl.Buffered(1))


def _layer_call(body, layer, x, mod, norm_g, w_in, w_out, in_width, params, scratch_shapes, name,
                stage_next_tile):
    batch, seq, d = x.shape
    tiles_per_seq = seq // SEQ_TILE
    n_tiles = batch * tiles_per_seq
    n_in = w_in.shape[1] // PREP_COLS
    n_out = w_out.shape[0] // PREP_ROWS
    n_prep = n_in + n_out

    def tile_index(t):
        return (t // tiles_per_seq, lax.rem(t, tiles_per_seq), 0)

    def tile_of_step(i):
        return jnp.maximum(i - n_prep, 0)

    tile = pl.BlockSpec((1, SEQ_TILE, d), lambda i: tile_index(tile_of_step(i)))
    x_specs, x_args = [tile], [x]
    if stage_next_tile:
        x_specs.append(pl.BlockSpec(
            (1, SEQ_TILE, d), lambda i: tile_index(jnp.minimum(tile_of_step(i) + 1, n_tiles - 1))))
        x_args.append(x)
    weight_specs = [
        pl.BlockSpec((w_in.shape[0], PREP_COLS), lambda i: (0, jnp.minimum(i, n_in - 1))),
        pl.BlockSpec((PREP_ROWS, w_out.shape[1]), lambda i: (jnp.clip(i - n_in, 0, n_out - 1), 0)),
    ]
    weight_scratch = [
        pltpu.VMEM((w_in.shape[1] // in_width, w_in.shape[0], in_width), jnp.bfloat16),
        pltpu.VMEM((w_out.shape[1] // OUT_COLS, w_out.shape[0], OUT_COLS), jnp.bfloat16),
        pltpu.VMEM((SEQ_TILE, d), jnp.bfloat16),
    ]
    return pl.pallas_call(
        functools.partial(body, layer, tiles_per_seq, n_tiles),
        grid=(n_prep + n_tiles,),
        in_specs=x_specs + [_resident(mod.shape, layer), _resident(norm_g.shape)] + weight_specs
                 + [_resident(p.shape) for p in params],
        out_specs=tile,
        out_shape=jax.ShapeDtypeStruct(x.shape, x.dtype),
        scratch_shapes=weight_scratch + scratch_shapes,
        compiler_params=pltpu.CompilerParams(
            dimension_semantics=("arbitrary",),
            vmem_limit_bytes=VMEM_LIMIT_BYTES),
        name=name,
    )(*x_args, mod, norm_g, w_in, w_out, *params)


def kernel(x, c, mod_w, mod_b, norm_g, a_w_in, a_conv_w, a_conv_b, a_w_out,
           b_w_in, b_ln_g, b_ln_b, b_w_s, b_b_s, b_w_out, final_g):
    batch, seq, d = x.shape
    e = D_INNER
    assert seq % SEQ_TILE == 0 and SEQ_TILE % CHUNK == 0 and d == D_MODEL
    assert mod_w.shape[0] == 2 and a_w_in.shape[0] == 1 and b_w_in.shape[0] == 1

    mod = _adaln_mod(c, mod_w, mod_b)
    bf16 = jnp.bfloat16
    x1 = _layer_call(
        _conv_layer_kernel, 0, x, mod, norm_g, a_w_in[0], a_w_out[0], CONV_COLS,
        [a_conv_w[0], a_conv_b],
        [pltpu.VMEM((SEQ_TILE + SUBLANES, e), jnp.float32),
         pltpu.VMEM((SEQ_TILE, e), bf16)],
        "conv_layer", stage_next_tile=True)
    return _layer_call(
        _gmlp_layer_kernel, 1, x1, mod, norm_g, b_w_in[0], b_w_out[0], GROUP_WIDTH,
        [b_ln_g, b_ln_b, b_w_s[0], b_b_s[0], final_g.reshape(1, d)],
        [pltpu.VMEM((SEQ_TILE, e), jnp.float32),
         pltpu.VMEM((SEQ_TILE, e), bf16)],
        "gmlp_layer", stage_next_tile=False)
```

```python
import functools
import math

import jax
import jax.numpy as jnp
from jax import lax
from jax.experimental import pallas as pl
from jax.experimental.pallas import tpu as pltpu

D_MODEL = 1024
D_INNER = 2048
CHUNK = 128
GMLP_GROUPS = 8
GROUP_WIDTH = D_INNER // GMLP_GROUPS
RMS_EPS = 1e-6
LN_EPS = 1e-5

SUBLANES = 8
LANES = 128
SEQ_TILE = 512
HEAD_ROW_BLOCKS = 4
TAIL_ROW_BLOCKS = 4
CONV_COLS = 256
OUT_COLS = 512
PREP_COLS = 1024
PREP_ROWS = 1024
VMEM_LIMIT_BYTES = 60 * 1024 * 1024

_SQRT_HALF = math.sqrt(0.5)


def _silu(v):
    return v * jax.nn.sigmoid(v)


def _gelu_exact(v):
    return 0.5 * v * (1.0 + lax.erf(v * _SQRT_HALF))


def _rms_norm(x, g):
    ms = jnp.mean(x * x, axis=-1, keepdims=True)
    return (x * lax.rsqrt(ms + RMS_EPS)) * g


def _bdot(a, b):
    return jnp.dot(a, b, preferred_element_type=jnp.float32)


def _mod_kernel(c_ref, w_ref, b_ref, o_ref):
    c_act = _silu(c_ref[...])
    o_ref[0, 0] = _bdot(c_act, w_ref[0]) + b_ref[pl.ds(pl.program_id(0), 1), :]


def _adaln_mod(c, mod_w, mod_b):
    depth, d, n = mod_w.shape
    batch = c.shape[0]
    return pl.pallas_call(
        _mod_kernel,
        grid=(depth, n // d),
        in_specs=[
            pl.BlockSpec((batch, d), lambda i, j: (0, 0)),
            pl.BlockSpec((1, d, d), lambda i, j: (i, 0, j)),
            pl.BlockSpec((depth, d), lambda i, j: (0, j)),
        ],
        out_specs=pl.BlockSpec((1, 1, batch, d), lambda i, j: (i, j, 0, 0)),
        out_shape=jax.ShapeDtypeStruct((depth, n // d, batch, d), jnp.float32),
        compiler_params=pltpu.CompilerParams(
            dimension_semantics=("arbitrary", "arbitrary"),
            vmem_limit_bytes=VMEM_LIMIT_BYTES),
        name="adaln_mod",
    )(c, mod_w, mod_b)


def _mod_row(mod_ref, which, b):
    return mod_ref[0, which, pl.ds(b, 1), :]


def _modulated(x_rows, mod_ref, g_row, b):
    gain = g_row * (1.0 + _mod_row(mod_ref, 1, b))
    ms = jnp.mean(x_rows * x_rows, axis=-1, keepdims=True)
    return (x_rows * lax.rsqrt(ms + RMS_EPS)) * gain + _mod_row(mod_ref, 0, b)


def _residual_out(x_ref, mod_ref, y_ref, wout_ref, b, rows=slice(None)):
    n_blocks, _, width = wout_ref.shape
    blocks = []
    for k in range(n_blocks):
        cols = slice(k * width, (k + 1) * width)
        blocks.append(x_ref[0, rows, cols] + _mod_row(mod_ref, 2, b)[:, cols] * _bdot(y_ref[rows, :], wout_ref[k]))
    return blocks


def _or_tree(parts):
    while len(parts) > 1:
        parts = [parts[k] | parts[k + 1] for k in range(0, len(parts), 2)]
    return parts[0]


def _zero_depending_on(a):
    rows, cols = a.shape
    u = lax.bitcast_convert_type(a, jnp.uint32)
    u = _or_tree([u[r:r + SUBLANES, :] for r in range(0, rows, SUBLANES)])
    u = _or_tree([u[:, c:c + LANES] for c in range(0, cols, LANES)])
    half = jnp.uint32(16)
    return lax.shift_right_logical(lax.shift_right_logical(u, half), half)[0:1, :]


def _tied(row, zero):
    n = row.shape[1]
    bits = lax.bitcast_convert_type(row, jnp.uint32) + jnp.tile(zero, (1, n // LANES))
    return lax.bitcast_convert_type(bits, jnp.float32)


def _next_h_parts(xn_ref, mod_ref, g_row, b_next, n_parts):
    rows = xn_ref.shape[1] // n_parts
    parts = []
    for q in range(n_parts):
        hq = _modulated(xn_ref[0, q * rows:(q + 1) * rows, :], mod_ref, g_row, b_next)
        parts.append((hq.astype(jnp.bfloat16), _zero_depending_on(hq)))
    return parts


def _store_next_h(h_ref, parts):
    rows = parts[0][0].shape[0]
    for q, (hq, _) in enumerate(parts):
        h_ref[q * rows:(q + 1) * rows, :] = hq


def _prepare_weights(win_blk_ref, wout_blk_ref, win_ref, wout_ref):
    i = pl.program_id(0)
    n_blocks, _, width = win_ref.shape
    per_step = win_blk_ref.shape[1] // width
    n_in = n_blocks // per_step
    out_blocks, k_out, out_width = wout_ref.shape
    rows = wout_blk_ref.shape[0]
    n_out = k_out // rows

    @pl.when(i < n_in)
    def _():
        for p in range(per_step):
            win_ref[i * per_step + p] = win_blk_ref[:, p * width:(p + 1) * width].astype(jnp.bfloat16)

    @pl.when(jnp.logical_and(i >= n_in, i < n_in + n_out))
    def _():
        r0 = pl.multiple_of((i - n_in) * rows, rows)
        for k in range(out_blocks):
            wout_ref[k, pl.ds(r0, rows), :] = (
                wout_blk_ref[:, k * out_width:(k + 1) * out_width].astype(jnp.bfloat16))

    return n_in + n_out


def _conv_layer_kernel(layer, tiles_per_seq, n_tiles, x_ref, xn_ref, mod_ref, g_ref, win_blk_ref, wout_blk_ref,
                       cw_ref, cb_ref, o_ref, win_ref, wout_ref, h_ref, cx_ref, y_ref):
    n_prep = _prepare_weights(win_blk_ref, wout_blk_ref, win_ref, wout_ref)

    @pl.when(pl.program_id(0) >= n_prep)
    def _():
        _conv_tile(tiles_per_seq, n_tiles, pl.program_id(0) - n_prep, x_ref, xn_ref, mod_ref,
                   g_ref[layer:layer + 1, :], win_ref, cw_ref, cb_ref, wout_ref, o_ref, h_ref, cx_ref, y_ref)


def _conv_tile(tiles_per_seq, n_tiles, i, x_ref, xn_ref, mod_ref, g_row, win_ref, cw_ref, cb_ref, wout_ref,
               o_ref, h_ref, cx_ref, y_ref):
    ts = x_ref.shape[1]
    e = D_INNER
    b = i // tiles_per_seq
    b_next = jnp.minimum(i + 1, n_tiles - 1) // tiles_per_seq

    @pl.when(i == 0)
    def _():
        h_ref[...] = _modulated(x_ref[0], mod_ref, g_row, 0).astype(jnp.bfloat16)

    @pl.when(lax.rem(i, tiles_per_seq) == 0)
    def _():
        cx_ref[0:SUBLANES, :] = jnp.zeros((SUBLANES, e), jnp.float32)

    n_chunks = e // CONV_COLS
    next_h = _next_h_parts(xn_ref, mod_ref, g_row, b_next, n_chunks)
    for j in range(n_chunks):
        cols = slice(j * CONV_COLS, (j + 1) * CONV_COLS)
        b_gate = _bdot(h_ref[...], win_ref[j])
        c_gate = _bdot(h_ref[...], win_ref[n_chunks + j])
        xin = _bdot(h_ref[...], win_ref[2 * n_chunks + j])
        z = _bdot(h_ref[...], win_ref[3 * n_chunks + j])
        cx = c_gate * xin
        cx_ref[SUBLANES:SUBLANES + ts, cols] = cx
        conv = _tied(cb_ref[:, cols], next_h[j][1]) + cw_ref[2:3, cols] * cx
        conv = conv + cw_ref[0:1, cols] * cx_ref[SUBLANES - 2:SUBLANES - 2 + ts, cols]
        conv = conv + cw_ref[1:2, cols] * cx_ref[SUBLANES - 1:SUBLANES - 1 + ts, cols]
        y_ref[:, cols] = (_silu(z) * b_gate * conv).astype(jnp.bfloat16)

    cx_ref[0:SUBLANES, :] = cx_ref[ts:ts + SUBLANES, :]
    _store_next_h(h_ref, next_h)
    width = wout_ref.shape[2]
    for k, blk in enumerate(_residual_out(x_ref, mod_ref, y_ref, wout_ref, b)):
        o_ref[0, :, k * width:(k + 1) * width] = blk


def _gmlp_layer_kernel(layer, tiles_per_seq, n_tiles, x_ref, mod_ref, g_ref, win_blk_ref, wout_blk_ref,
                       lng_ref, lnb_ref, ws_ref, bs_ref, fg_ref, o_ref, win_ref, wout_ref, h_ref, v_ref, y_ref):
    n_prep = _prepare_weights(win_blk_ref, wout_blk_ref, win_ref, wout_ref)

    @pl.when(pl.program_id(0) >= n_prep)
    def _():
        b = (pl.program_id(0) - n_prep) // tiles_per_seq
        _gmlp_tile(b, x_ref, mod_ref, g_ref[layer:layer + 1, :], win_ref, lng_ref, lnb_ref, ws_ref, bs_ref,
                   wout_ref, fg_ref, o_ref, h_ref, v_ref, y_ref)


def _gmlp_tile(b, x_ref, mod_ref, g_row, win_ref, lng_ref, lnb_ref, ws_ref, bs_ref, wout_ref, fg_ref,
               o_ref, h_ref, v_ref, y_ref):
    ts = x_ref.shape[1]
    e = D_INNER
    gw = GROUP_WIDTH

    h_ref[...] = _modulated(x_ref[0], mod_ref, g_row, b).astype(jnp.bfloat16)

    d_sum = jnp.zeros((ts, 1), jnp.float32)
    d_sq = jnp.zeros((ts, 1), jnp.float32)
    for g in range(GMLP_GROUPS):
        if g == 0:
            rb = ts // HEAD_ROW_BLOCKS
            v = jnp.concatenate([_gelu_exact(_bdot(h_ref[r * rb:(r + 1) * rb, :], win_ref[GMLP_GROUPS]))
                                 for r in range(HEAD_ROW_BLOCKS)], axis=0)
        else:
            v = _gelu_exact(_bdot(h_ref[...], win_ref[GMLP_GROUPS + g]))
        v_ref[:, g * gw:(g + 1) * gw] = v
        if g == 0:
            pivot = jnp.mean(v, axis=-1, keepdims=True)
        d = v - pivot
        d_sum = d_sum + jnp.sum(d, axis=-1, keepdims=True)
        d_sq = d_sq + jnp.sum(d * d, axis=-1, keepdims=True)
    d_mean = d_sum * (1.0 / e)
    mu = pivot + d_mean
    rstd = lax.rsqrt(d_sq * (1.0 / e) - d_mean * d_mean + LN_EPS)

    row = lax.broadcasted_iota(jnp.int32, (CHUNK, CHUNK), 0)
    col = lax.broadcasted_iota(jnp.int32, (CHUNK, CHUNK), 1)
    causal = col <= row
    diagonal = col == row

    for g in range(GMLP_GROUPS):
        cols = slice(g * gw, (g + 1) * gw)
        u = _gelu_exact(_bdot(h_ref[...], win_ref[g]))
        z = _bdot(h_ref[...], win_ref[2 * GMLP_GROUPS + g])
        vn = ((v_ref[:, cols] - mu) * rstd * lng_ref[:, cols] + lnb_ref[:, cols]).astype(jnp.bfloat16)
        w_g = jnp.where(causal, ws_ref[g], 0.0).astype(jnp.bfloat16)
        b_g = jnp.sum(jnp.where(diagonal, bs_ref[g:g + 1, :], 0.0), axis=1, keepdims=True)
        mixed = jnp.concatenate(
            [_bdot(w_g, vn[n * CHUNK:(n + 1) * CHUNK, :]) + b_g for n in range(ts // CHUNK)], axis=0)
        y_ref[:, cols] = (_silu(z) * (u * mixed)).astype(jnp.bfloat16)

    rb = ts // TAIL_ROW_BLOCKS
    for r in range(TAIL_ROW_BLOCKS):
        rows = slice(r * rb, (r + 1) * rb)
        x_out = jnp.concatenate(_residual_out(x_ref, mod_ref, y_ref, wout_ref, b, rows), axis=1)
        o_ref[0, rows, :] = _rms_norm(x_out, fg_ref[...])


def _resident(shape, lead=None):
    if lead is None:
        return pl.BlockSpec(tuple(shape), lambda i: (0,) * len(shape), pipeline_mode=pl.Buffered(1))
    return pl.BlockSpec((1,) + tuple(shape[1:]), lambda i: (lead,) + (0,) * (len(shape) - 1),
                        pipeline_mode=pl.Buffered(1))


def _layer_call(body, layer, x, mod, norm_g, w_in, w_out, in_width, params, scratch_shapes, name,
                stage_next_tile):
    batch, seq, d = x.shape
    tiles_per_seq = seq // SEQ_TILE
    n_tiles = batch * tiles_per_seq
    n_in = w_in.shape[1] // PREP_COLS
    n_out = w_out.shape[0] // PREP_ROWS
    n_prep = n_in + n_out

    def tile_index(t):
        return (t // tiles_per_seq, lax.rem(t, tiles_per_seq), 0)

    def tile_of_step(i):
        return jnp.maximum(i - n_prep, 0)

    tile = pl.BlockSpec((1, SEQ_TILE, d), lambda i: tile_index(tile_of_step(i)))
    x_specs, x_args = [tile], [x]
    if stage_next_tile:
        x_specs.append(pl.BlockSpec(
            (1, SEQ_TILE, d), lambda i: tile_index(jnp.minimum(tile_of_step(i) + 1, n_tiles - 1))))
        x_args.append(x)
    weight_specs = [
        pl.BlockSpec((w_in.shape[0], PREP_COLS), lambda i: (0, jnp.minimum(i, n_in - 1))),
        pl.BlockSpec((PREP_ROWS, w_out.shape[1]), lambda i: (jnp.clip(i - n_in, 0, n_out - 1), 0)),
    ]
    weight_scratch = [
        pltpu.VMEM((w_in.shape[1] // in_width, w_in.shape[0], in_width), jnp.bfloat16),
        pltpu.VMEM((w_out.shape[1] // OUT_COLS, w_out.shape[0], OUT_COLS), jnp.bfloat16),
        pltpu.VMEM((SEQ_TILE, d), jnp.bfloat16),
    ]
    return pl.pallas_call(
        functools.partial(body, layer, tiles_per_seq, n_tiles),
        grid=(n_prep + n_tiles,),
        in_specs=x_specs + [_resident(mod.shape, layer), _resident(norm_g.shape)] + weight_specs
                 + [_resident(p.shape) for p in params],
        out_specs=tile,
        out_shape=jax.ShapeDtypeStruct(x.shape, x.dtype),
        scratch_shapes=weight_scratch + scratch_shapes,
        compiler_params=pltpu.CompilerParams(
            dimension_semantics=("arbitrary",),
            vmem_limit_bytes=VMEM_LIMIT_BYTES),
        name=name,
    )(*x_args, mod, norm_g, w_in, w_out, *params)


def kernel(x, c, mod_w, mod_b, norm_g, a_w_in, a_conv_w, a_conv_b, a_w_out,
           b_w_in, b_ln_g, b_ln_b, b_w_s, b_b_s, b_w_out, final_g):
    batch, seq, d = x.shape
    e = D_INNER
    assert seq % SEQ_TILE == 0 and SEQ_TILE % CHUNK == 0 and d == D_MODEL
    assert mod_w.shape[0] == 2 and a_w_in.shape[0] == 1 and b_w_in.shape[0] == 1

    mod = _adaln_mod(c, mod_w, mod_b)
    bf16 = jnp.bfloat16
    x1 = _layer_call(
        _conv_layer_kernel, 0, x, mod, norm_g, a_w_in[0], a_w_out[0], CONV_COLS,
        [a_conv_w[0], a_conv_b],
        [pltpu.VMEM((SEQ_TILE + SUBLANES, e), jnp.float32),
         pltpu.VMEM((SEQ_TILE, e), bf16)],
        "conv_layer", stage_next_tile=True)
    return _layer_call(
        _gmlp_layer_kernel, 1, x1, mod, norm_g, b_w_in[0], b_w_out[0], GROUP_WIDTH,
        [b_ln_g, b_ln_b, b_w_s[0], b_b_s[0], final_g.reshape(1, d)],
        [pltpu.VMEM((SEQ_TILE, e), jnp.float32),
         pltpu.VMEM((SEQ_TILE, e), bf16)],
        "gmlp_layer", stage_next_tile=False)
```

```python
import functools
import math

import jax
import jax.numpy as jnp
from jax import lax
from jax.experimental import pallas as pl
from jax.experimental.pallas import tpu as pltpu

D_MODEL = 1024
D_INNER = 2048
CHUNK = 128
GMLP_GROUPS = 8
GROUP_WIDTH = D_INNER // GMLP_GROUPS
RMS_EPS = 1e-6
LN_EPS = 1e-5

SUBLANES = 8
LANES = 128
SEQ_TILE = 512
HEAD_ROW_BLOCKS = 4
TAIL_ROW_BLOCKS = 4
CONV_COLS = 256
OUT_COLS = 512
PREP_COLS = 1024
PREP_ROWS = 1024
VMEM_LIMIT_BYTES = 60 * 1024 * 1024

_SQRT_HALF = math.sqrt(0.5)


def _silu(v):
    return v * jax.nn.sigmoid(v)


def _gelu_exact(v):
    return 0.5 * v * (1.0 + lax.erf(v * _SQRT_HALF))


def _rms_norm(x, g):
    ms = jnp.mean(x * x, axis=-1, keepdims=True)
    return (x * lax.rsqrt(ms + RMS_EPS)) * g


def _bdot(a, b):
    return jnp.dot(a, b, preferred_element_type=jnp.float32)


def _mod_kernel(c_ref, w_ref, b_ref, o_ref):
    c_act = _silu(c_ref[...])
    o_ref[0, 0] = _bdot(c_act, w_ref[0]) + b_ref[pl.ds(pl.program_id(0), 1), :]


def _adaln_mod(c, mod_w, mod_b):
    depth, d, n = mod_w.shape
    batch = c.shape[0]
    return pl.pallas_call(
        _mod_kernel,
        grid=(depth, n // d),
        in_specs=[
            pl.BlockSpec((batch, d), lambda i, j: (0, 0)),
            pl.BlockSpec((1, d, d), lambda i, j: (i, 0, j)),
            pl.BlockSpec((depth, d), lambda i, j: (0, j)),
        ],
        out_specs=pl.BlockSpec((1, 1, batch, d), lambda i, j: (i, j, 0, 0)),
        out_shape=jax.ShapeDtypeStruct((depth, n // d, batch, d), jnp.float32),
        compiler_params=pltpu.CompilerParams(
            dimension_semantics=("arbitrary", "arbitrary"),
            vmem_limit_bytes=VMEM_LIMIT_BYTES),
        name="adaln_mod",
    )(c, mod_w, mod_b)


def _mod_row(mod_ref, which, b):
    return mod_ref[0, which, pl.ds(b, 1), :]


def _modulated(x_rows, mod_ref, g_row, b):
    gain = g_row * (1.0 + _mod_row(mod_ref, 1, b))
    ms = jnp.mean(x_rows * x_rows, axis=-1, keepdims=True)
    return (x_rows * lax.rsqrt(ms + RMS_EPS)) * gain + _mod_row(mod_ref, 0, b)


def _residual_out(x_ref, mod_ref, y_ref, wout_ref, b, rows=slice(None)):
    n_blocks, _, width = wout_ref.shape
    blocks = []
    for k in range(n_blocks):
        cols = slice(k * width, (k + 1) * width)
        blocks.append(x_ref[0, rows, cols] + _mod_row(mod_ref, 2, b)[:, cols] * _bdot(y_ref[rows, :], wout_ref[k]))
    return blocks


def _or_tree(parts):
    while len(parts) > 1:
        parts = [parts[k] | parts[k + 1] for k in range(0, len(parts), 2)]
    return parts[0]


def _zero_depending_on(a):
    rows, cols = a.shape
    u = lax.bitcast_convert_type(a, jnp.uint32)
    u = _or_tree([u[r:r + SUBLANES, :] for r in range(0, rows, SUBLANES)])
    u = _or_tree([u[:, c:c + LANES] for c in range(0, cols, LANES)])
    half = jnp.uint32(16)
    return lax.shift_right_logical(lax.shift_right_logical(u, half), half)[0:1, :]


def _tied(row, zero):
    n = row.shape[1]
    bits = lax.bitcast_convert_type(row, jnp.uint32) + jnp.tile(zero, (1, n // LANES))
    return lax.bitcast_convert_type(bits, jnp.float32)


def _next_h_parts(xn_ref, mod_ref, g_row, b_next, n_parts):
    rows = xn_ref.shape[1] // n_parts
    parts = []
    for q in range(n_parts):
        hq = _modulated(xn_ref[0, q * rows:(q + 1) * rows, :], mod_ref, g_row, b_next)
        parts.append((hq.astype(jnp.bfloat16), _zero_depending_on(hq)))
    return parts


def _store_next_h(h_ref, parts):
    rows = parts[0][0].shape[0]
    for q, (hq, _) in enumerate(parts):
        h_ref[q * rows:(q + 1) * rows, :] = hq


def _prepare_weights(win_blk_ref, wout_blk_ref, win_ref, wout_ref):
    i = pl.program_id(0)
    n_blocks, _, width = win_ref.shape
    per_step = win_blk_ref.shape[1] // width
    n_in = n_blocks // per_step
    out_blocks, k_out, out_width = wout_ref.shape
    rows = wout_blk_ref.shape[0]
    n_out = k_out // rows

    @pl.when(i < n_in)
    def _():
        for p in range(per_step):
            win_ref[i * per_step + p] = win_blk_ref[:, p * width:(p + 1) * width].astype(jnp.bfloat16)

    @pl.when(i < n_out)
    def _():
        r0 = pl.multiple_of(i * rows, rows)
        for k in range(out_blocks):
            wout_ref[k, pl.ds(r0, rows), :] = (
                wout_blk_ref[:, k * out_width:(k + 1) * out_width].astype(jnp.bfloat16))

    return max(n_in, n_out)


def _conv_layer_kernel(layer, tiles_per_seq, n_tiles, x_ref, xn_ref, mod_ref, g_ref, win_blk_ref, wout_blk_ref,
                       cw_ref, cb_ref, o_ref, win_ref, wout_ref, h_ref, cx_ref, y_ref):
    n_prep = _prepare_weights(win_blk_ref, wout_blk_ref, win_ref, wout_ref)

    @pl.when(pl.program_id(0) >= n_prep)
    def _():
        _conv_tile(tiles_per_seq, n_tiles, pl.program_id(0) - n_prep, x_ref, xn_ref, mod_ref,
                   g_ref[layer:layer + 1, :], win_ref, cw_ref, cb_ref, wout_ref, o_ref, h_ref, cx_ref, y_ref)


def _conv_tile(tiles_per_seq, n_tiles, i, x_ref, xn_ref, mod_ref, g_row, win_ref, cw_ref, cb_ref, wout_ref,
               o_ref, h_ref, cx_ref, y_ref):
    ts = x_ref.shape[1]
    e = D_INNER
    b = i // tiles_per_seq
    b_next = jnp.minimum(i + 1, n_tiles - 1) // tiles_per_seq

    @pl.when(i == 0)
    def _():
        h_ref[...] = _modulated(x_ref[0], mod_ref, g_row, 0).astype(jnp.bfloat16)

    @pl.when(lax.rem(i, tiles_per_seq) == 0)
    def _():
        cx_ref[0:SUBLANES, :] = jnp.zeros((SUBLANES, e), jnp.float32)

    n_chunks = e // CONV_COLS
    next_h = _next_h_parts(xn_ref, mod_ref, g_row, b_next, n_chunks)
    for j in range(n_chunks):
        cols = slice(j * CONV_COLS, (j + 1) * CONV_COLS)
        c_gate = _bdot(h_ref[...], win_ref[n_chunks + j])
        xin = _bdot(h_ref[...], win_ref[2 * n_chunks + j])
        cx = c_gate * xin
        cx_ref[SUBLANES:SUBLANES + ts, cols] = cx
        conv = _tied(cb_ref[:, cols], next_h[j][1]) + cw_ref[2:3, cols] * cx
        conv = conv + cw_ref[0:1, cols] * cx_ref[SUBLANES - 2:SUBLANES - 2 + ts, cols]
        conv = conv + cw_ref[1:2, cols] * cx_ref[SUBLANES - 1:SUBLANES - 1 + ts, cols]
        z = _bdot(h_ref[...], win_ref[3 * n_chunks + j])
        gated = _silu(z) * conv
        b_gate = _bdot(h_ref[...], win_ref[j])
        y_ref[:, cols] = (gated * b_gate).astype(jnp.bfloat16)

    cx_ref[0:SUBLANES, :] = cx_ref[ts:ts + SUBLANES, :]
    _store_next_h(h_ref, next_h)
    width = wout_ref.shape[2]
    for k, blk in enumerate(_residual_out(x_ref, mod_ref, y_ref, wout_ref, b)):
        o_ref[0, :, k * width:(k + 1) * width] = blk


def _gmlp_layer_kernel(layer, tiles_per_seq, n_tiles, x_ref, mod_ref, g_ref, win_blk_ref, wout_blk_ref,
                       lng_ref, lnb_ref, ws_ref, bs_ref, fg_ref, o_ref, win_ref, wout_ref, h_ref, v_ref, y_ref):
    n_prep = _prepare_weights(win_blk_ref, wout_blk_ref, win_ref, wout_ref)

    @pl.when(pl.program_id(0) >= n_prep)
    def _():
        b = (pl.program_id(0) - n_prep) // tiles_per_seq
        _gmlp_tile(b, x_ref, mod_ref, g_ref[layer:layer + 1, :], win_ref, lng_ref, lnb_ref, ws_ref, bs_ref,
                   wout_ref, fg_ref, o_ref, h_ref, v_ref, y_ref)


def _gmlp_tile(b, x_ref, mod_ref, g_row, win_ref, lng_ref, lnb_ref, ws_ref, bs_ref, wout_ref, fg_ref,
               o_ref, h_ref, v_ref, y_ref):
    ts = x_ref.shape[1]
    e = D_INNER
    gw = GROUP_WIDTH

    h_ref[...] = _modulated(x_ref[0], mod_ref, g_row, b).astype(jnp.bfloat16)

    d_sum = jnp.zeros((ts, 1), jnp.float32)
    d_sq = jnp.zeros((ts, 1), jnp.float32)
    for g in range(GMLP_GROUPS):
        if g == 0:
            rb = ts // HEAD_ROW_BLOCKS
            v = jnp.concatenate([_gelu_exact(_bdot(h_ref[r * rb:(r + 1) * rb, :], win_ref[GMLP_GROUPS]))
                                 for r in range(HEAD_ROW_BLOCKS)], axis=0)
        else:
            v = _gelu_exact(_bdot(h_ref[...], win_ref[GMLP_GROUPS + g]))
        v_ref[:, g * gw:(g + 1) * gw] = v
        if g == 0:
            pivot = jnp.mean(v, axis=-1, keepdims=True)
        d = v - pivot
        d_sum = d_sum + jnp.sum(d, axis=-1, keepdims=True)
        d_sq = d_sq + jnp.sum(d * d, axis=-1, keepdims=True)
    d_mean = d_sum * (1.0 / e)
    mu = pivot + d_mean
    rstd = lax.rsqrt(d_sq * (1.0 / e) - d_mean * d_mean + LN_EPS)

    row = lax.broadcasted_iota(jnp.int32, (CHUNK, CHUNK), 0)
    col = lax.broadcasted_iota(jnp.int32, (CHUNK, CHUNK), 1)
    causal = col <= row
    diagonal = col == row

    for g in range(GMLP_GROUPS):
        cols = slice(g * gw, (g + 1) * gw)
        u = _gelu_exact(_bdot(h_ref[...], win_ref[g]))
        z = _bdot(h_ref[...], win_ref[2 * GMLP_GROUPS + g])
        vn = ((v_ref[:, cols] - mu) * rstd * lng_ref[:, cols] + lnb_ref[:, cols]).astype(jnp.bfloat16)
        w_g = jnp.where(causal, ws_ref[g], 0.0).astype(jnp.bfloat16)
        b_g = jnp.sum(jnp.where(diagonal, bs_ref[g:g + 1, :], 0.0), axis=1, keepdims=True)
        mixed = jnp.concatenate(
            [_bdot(w_g, vn[n * CHUNK:(n + 1) * CHUNK, :]) + b_g for n in range(ts // CHUNK)], axis=0)
        y_ref[:, cols] = (_silu(z) * (u * mixed)).astype(jnp.bfloat16)

    rb = ts // TAIL_ROW_BLOCKS
    for r in range(TAIL_ROW_BLOCKS):
        rows = slice(r * rb, (r + 1) * rb)
        x_out = jnp.concatenate(_residual_out(x_ref, mod_ref, y_ref, wout_ref, b, rows), axis=1)
        o_ref[0, rows, :] = _rms_norm(x_out, fg_ref[...])


def _resident(shape, lead=None):
    if lead is None:
        return pl.BlockSpec(tuple(shape), lambda i: (0,) * len(shape), pipeline_mode=pl.Buffered(1))
    return pl.BlockSpec((1,) + tuple(shape[1:]), lambda i: (lead,) + (0,) * (len(shape) - 1),
                        pipeline_mode=pl.Buffered(1))


def _layer_call(body, layer, x, mod, norm_g, w_in, w_out, in_width, params, scratch_shapes, name,
                stage_next_tile):
    batch, seq, d = x.shape
    tiles_per_seq = seq // SEQ_TILE
    n_tiles = batch * tiles_per_seq
    n_in = w_in.shape[1] // PREP_COLS
    n_out = w_out.shape[0] // PREP_ROWS
    n_prep = max(n_in, n_out)

    def tile_index(t):
        return (t // tiles_per_seq, lax.rem(t, tiles_per_seq), 0)

    def tile_of_step(i):
        return jnp.maximum(i - n_prep, 0)

    tile = pl.BlockSpec((1, SEQ_TILE, d), lambda i: tile_index(tile_of_step(i)))
    x_specs, x_args = [tile], [x]
    if stage_next_tile:
        x_specs.append(pl.BlockSpec(
            (1, SEQ_TILE, d), lambda i: tile_index(jnp.minimum(tile_of_step(i) + 1, n_tiles - 1))))
        x_args.append(x)
    weight_specs = [
        pl.BlockSpec((w_in.shape[0], PREP_COLS), lambda i: (0, jnp.minimum(i, n_in - 1))),
        pl.BlockSpec((PREP_ROWS, w_out.shape[1]), lambda i: (jnp.minimum(i, n_out - 1), 0)),
    ]
    weight_scratch = [
        pltpu.VMEM((w_in.shape[1] // in_width, w_in.shape[0], in_width), jnp.bfloat16),
        pltpu.VMEM((w_out.shape[1] // OUT_COLS, w_out.shape[0], OUT_COLS), jnp.bfloat16),
        pltpu.VMEM((SEQ_TILE, d), jnp.bfloat16),
    ]
    return pl.pallas_call(
        functools.partial(body, layer, tiles_per_seq, n_tiles),
        grid=(n_prep + n_tiles,),
        in_specs=x_specs + [_resident(mod.shape, layer), _resident(norm_g.shape)] + weight_specs
                 + [_resident(p.shape) for p in params],
        out_specs=tile,
        out_shape=jax.ShapeDtypeStruct(x.shape, x.dtype),
        scratch_shapes=weight_scratch + scratch_shapes,
        compiler_params=pltpu.CompilerParams(
            dimension_semantics=("arbitrary",),
            vmem_limit_bytes=VMEM_LIMIT_BYTES),
        name=name,
    )(*x_args, mod, norm_g, w_in, w_out, *params)


def kernel(x, c, mod_w, mod_b, norm_g, a_w_in, a_conv_w, a_conv_b, a_w_out,
           b_w_in, b_ln_g, b_ln_b, b_w_s, b_b_s, b_w_out, final_g):
    batch, seq, d = x.shape
    e = D_INNER
    assert seq % SEQ_TILE == 0 and SEQ_TILE % CHUNK == 0 and d == D_MODEL
    assert mod_w.shape[0] == 2 and a_w_in.shape[0] == 1 and b_w_in.shape[0] == 1

    mod = _adaln_mod(c, mod_w, mod_b)
    bf16 = jnp.bfloat16
    x1 = _layer_call(
        _conv_layer_kernel, 0, x, mod, norm_g, a_w_in[0], a_w_out[0], CONV_COLS,
        [a_conv_w[0], a_conv_b],
        [pltpu.VMEM((SEQ_TILE + SUBLANES, e), jnp.float32),
         pltpu.VMEM((SEQ_TILE, e), bf16)],
        "conv_layer", stage_next_tile=True)
    return _layer_call(
        _gmlp_layer_kernel, 1, x1, mod, norm_g, b_w_in[0], b_w_out[0], GROUP_WIDTH,
        [b_ln_g, b_ln_b, b_w_s[0], b_b_s[0], final_g.reshape(1, d)],
        [pltpu.VMEM((SEQ_TILE, e), jnp.float32),
         pltpu.VMEM((SEQ_TILE, e), bf16)],
        "gmlp_layer", stage_next_tile=False)
```
